```python
import jax, jax.numpy as jnp
from jax import lax
import numpy as np

D_MODEL = 1024
BATCH = 2
SEQ = 8192
DEPTH = 1
DEC_BATCH = 8
DEC_SEQ = 64
PAST_LEN = 1024

CHUNK = 64
SSD_EXPAND = 2
D_INNER = SSD_EXPAND * D_MODEL
SSD_HEAD_DIM = 64
SSD_HEADS = D_INNER // SSD_HEAD_DIM
SSD_GROUPS = 8
SSD_HPG = SSD_HEADS // SSD_GROUPS
D_STATE = 128
CONV_W = 4
CONV_DIM = D_INNER + 2 * SSD_GROUPS * D_STATE
D_POOL = D_MODEL
POOL_WINDOWS = (2, 4, 8, 16)
POOL_GROUPS = len(POOL_WINDOWS)
POOL_GDIM = D_POOL // POOL_GROUPS
POOL_MAX = max(POOL_WINDOWS)
N_BRANCH = 2
SPLIT_POINTS = (
    D_INNER,
    D_INNER + CONV_DIM,
    D_INNER + CONV_DIM + SSD_HEADS,
    D_INNER + CONV_DIM + SSD_HEADS + D_POOL,
    D_INNER + CONV_DIM + SSD_HEADS + 2 * D_POOL,
)
IN_COLS = SPLIT_POINTS[-1] + N_BRANCH * D_MODEL
EPS = 1e-6

kernel_name = "hybrid_ssd_pool_streaming_step"


def rmsnorm(x, g):
    xf = x.astype(jnp.float32)
    y = xf * lax.rsqrt(jnp.mean(xf * xf, axis=-1, keepdims=True) + EPS)
    return (y * g.astype(jnp.float32)).astype(x.dtype)


def gated_group_rmsnorm(y, z, g):
    yf = (y.astype(jnp.float32) * jax.nn.silu(z.astype(jnp.float32)))
    shp = yf.shape
    yg = yf.reshape(shp[:-1] + (SSD_GROUPS, D_INNER // SSD_GROUPS))
    yg = yg * lax.rsqrt(jnp.mean(yg * yg, axis=-1, keepdims=True) + EPS)
    return (yg.reshape(shp) * g.astype(jnp.float32)).astype(z.dtype)


def causal_dwconv(x_ext, w, b):
    out = lax.conv_general_dilated(
        x_ext, w[:, None, :].astype(x_ext.dtype), window_strides=(1,), padding='VALID',
        dimension_numbers=('NWC', 'WIO', 'NWC'), feature_group_count=x_ext.shape[-1])
    return out + b.astype(x_ext.dtype)


def ssd_chunk(h, x, dt, B, C, A):
    l = x.shape[1]
    acum = jnp.cumsum(dt * A, axis=1)
    causal = jnp.tril(jnp.ones((l, l), dtype=bool))[None, :, :, None, None]
    diff = acum[:, :, None] - acum[:, None, :]
    L = jnp.exp(jnp.where(causal, diff, -jnp.inf))
    CB = jnp.einsum('blgn,bsgn->blsg', C, B)
    xdt = x * dt[..., None]
    y_diag = jnp.einsum('blsgr,bsgrp->blgrp', CB[..., None] * L, xdt)
    y_off = jnp.einsum('blgn,bgrpn->blgrp', C, h) * jnp.exp(acum)[..., None]
    decay_end = jnp.exp(acum[:, -1:] - acum)
    h_new = (h * jnp.exp(acum[:, -1])[..., None, None]
             + jnp.einsum('blgn,blgrp->bgrpn', B, xdt * decay_end[..., None]))
    return y_diag + y_off, h_new


def ssd_scan(h, x, dt, B, C, A):
    b, t = x.shape[:2]
    if t <= CHUNK:
        return ssd_chunk(h, x, dt, B, C, A)
    nc = t // CHUNK

    def blocks(a):
        return jnp.swapaxes(a.reshape((b, nc, CHUNK) + a.shape[2:]), 0, 1)

    def step(carry, inp):
        xc, dtc, bc, cc = inp
        y, carry = ssd_chunk(carry, xc, dtc, bc, cc, A)
        return carry, y

    h_fin, ys = lax.scan(step, h, (blocks(x), blocks(dt), blocks(B), blocks(C)))
    y = jnp.swapaxes(ys, 0, 1).reshape((b, t) + x.shape[2:])
    return y, h_fin


def pool_mix(u_ext, pos0, pool_w, pool_scale):
    f32 = jnp.float32
    P = POOL_MAX - 1
    b, tot, _ = u_ext.shape
    t = tot - P
    uf = u_ext.astype(f32)
    cs = jnp.concatenate([jnp.zeros((b, 1, D_POOL), f32), jnp.cumsum(uf, axis=1)], axis=1)
    end = cs[:, P + 1:]
    cur = uf[:, P:]
    pos = pos0 + jnp.arange(t)
    outs = []
    for gi, w in enumerate(POOL_WINDOWS):
        sl = slice(gi * POOL_GDIM, (gi + 1) * POOL_GDIM)
        win = end[:, :, sl] - cs[:, P + 1 - w:P + 1 - w + t, sl]
        cnt = jnp.minimum(w, pos + 1).astype(f32)[None, :, None]
        outs.append(win / cnt - cur[:, :, sl])
    d = jnp.stack(outs, axis=2)
    o = jnp.einsum('btgc,gcd->btgd', d, pool_w.astype(f32)).reshape(b, t, D_POOL)
    return (o * pool_scale.astype(f32)).astype(u_ext.dtype)


def mixer_layer(x, conv_prev, pool_prev, h0, pos0, norm_g, w_in, conv_w, conv_b, dt_bias, a_log,
                d_skip, ssd_norm_g, w_branch_ssd, pool_w, pool_scale, w_branch_pool, gate_b, w_out):
    f32 = jnp.float32
    b, t, _ = x.shape
    n = rmsnorm(x, norm_g)
    proj = jnp.einsum('btd,de->bte', n, w_in)
    z, xbc, dt_raw, u, pool_gate, gate_logits = jnp.split(proj, SPLIT_POINTS, axis=-1)
    xbc_ext = jnp.concatenate([conv_prev.astype(xbc.dtype), xbc], axis=1)
    xbc_c = jax.nn.silu(causal_dwconv(xbc_ext, conv_w, conv_b))
    xs, bs, cs = jnp.split(xbc_c, [D_INNER, D_INNER + SSD_GROUPS * D_STATE], axis=-1)
    xs = xs.astype(f32).reshape(b, t, SSD_GROUPS, SSD_HPG, SSD_HEAD_DIM)
    bs = bs.astype(f32).reshape(b, t, SSD_GROUPS, D_STATE)
    cs = cs.astype(f32).reshape(b, t, SSD_GROUPS, D_STATE)
    dt = jax.nn.softplus(dt_raw.astype(f32) + dt_bias.astype(f32)).reshape(b, t, SSD_GROUPS, SSD_HPG)
    A = -jnp.exp(a_log.astype(f32)).reshape(SSD_GROUPS, SSD_HPG)
    h_init = h0.astype(f32).reshape(b, SSD_GROUPS, SSD_HPG, SSD_HEAD_DIM, D_STATE)
    y, h_fin = ssd_scan(h_init, xs, dt, bs, cs, A)
    y = y + d_skip.astype(f32).reshape(SSD_GROUPS, SSD_HPG)[:, :, None] * xs
    y = y.reshape(b, t, D_INNER)
    br_ssd = jnp.einsum('bti,id->btd', gated_group_rmsnorm(y, z, ssd_norm_g), w_branch_ssd)
    u_ext = jnp.concatenate([pool_prev.astype(u.dtype), u], axis=1)
    pooled = pool_mix(u_ext, pos0, pool_w, pool_scale)
    br_pool = jnp.einsum('btc,cd->btd', pooled * jax.nn.silu(pool_gate), w_branch_pool)
    gates = jax.nn.sigmoid((gate_logits + gate_b).astype(f32)).astype(x.dtype)
    g_ssd, g_pool = jnp.split(gates, N_BRANCH, axis=-1)
    out = jnp.einsum('btd,de->bte', g_ssd * br_ssd + g_pool * br_pool, w_out)
    new_h = h_fin.reshape(b, SSD_HEADS, SSD_HEAD_DIM, D_STATE)
    return x + out, xbc_ext[:, -(CONV_W - 1):], u_ext[:, -(POOL_MAX - 1):], new_h


def setup_inputs(seed: int = 0) -> dict:
    key = jax.random.key(seed)
    ks = jax.random.split(key, 24)
    f32 = jnp.float32
    nrm = lambda k, s: jax.random.normal(k, s, f32)
    dt0 = jnp.exp(jax.random.uniform(ks[7], (DEPTH, SSD_HEADS), f32, np.log(1e-3), np.log(1e-1)))
    return {
        "x_prompt": nrm(ks[0], (BATCH, SEQ, D_MODEL)),
        "x_sample": nrm(ks[1], (DEC_BATCH, DEC_SEQ, D_MODEL)),
        "state_ssd": 0.1 * nrm(ks[2], (DEPTH, DEC_BATCH, SSD_HEADS, SSD_HEAD_DIM, D_STATE)),
        "state_conv": nrm(ks[3], (DEPTH, DEC_BATCH, CONV_W - 1, CONV_DIM)),
        "state_pool": nrm(ks[4], (DEPTH, DEC_BATCH, POOL_MAX - 1, D_POOL)),
        "norm_g": 1.0 + 0.02 * nrm(ks[5], (DEPTH, D_MODEL)),
        "w_in": nrm(ks[6], (DEPTH, D_MODEL, IN_COLS)) * D_MODEL ** -0.5,
        "conv_w": 0.5 * nrm(ks[8], (DEPTH, CONV_W, CONV_DIM)),
        "conv_b": 0.02 * nrm(ks[9], (DEPTH, CONV_DIM)),
        "dt_bias": dt0 + jnp.log(-jnp.expm1(-dt0)),
        "a_log": jnp.log(jax.random.uniform(ks[10], (DEPTH, SSD_HEADS), f32, 1.0, 16.0)),
        "d_skip": 1.0 + 0.1 * nrm(ks[11], (DEPTH, SSD_HEADS)),
        "ssd_norm_g": 1.0 + 0.02 * nrm(ks[12], (DEPTH, D_INNER)),
        "w_branch_ssd": nrm(ks[13], (DEPTH, D_INNER, D_MODEL)) * D_INNER ** -0.5,
        "pool_w": nrm(ks[14], (DEPTH, POOL_GROUPS, POOL_GDIM, POOL_GDIM)) * POOL_GDIM ** -0.5,
        "pool_scale": 1.0 + 0.1 * nrm(ks[15], (DEPTH, D_POOL)),
        "w_branch_pool": nrm(ks[16], (DEPTH, D_POOL, D_MODEL)) * D_POOL ** -0.5,
        "gate_b": 0.02 * nrm(ks[17], (DEPTH, N_BRANCH * D_MODEL)),
        "w_out": nrm(ks[18], (DEPTH, D_MODEL, D_MODEL)) * D_MODEL ** -0.5,
        "final_g": 1.0 + 0.02 * nrm(ks[19], (D_MODEL,)),
    }


def reference(x_prompt, x_sample, state_ssd, state_conv, state_pool, norm_g, w_in, conv_w, conv_b,
              dt_bias, a_log, d_skip, ssd_norm_g, w_branch_ssd, pool_w, pool_scale, w_branch_pool,
              gate_b, w_out, final_g):
    bp = x_prompt.shape[0]
    hp, hs = x_prompt, x_sample
    ssd_p, ssd_s, conv_p, conv_s, pool_p, pool_s = [], [], [], [], [], []
    for l in range(DEPTH):
        lw = (norm_g[l], w_in[l], conv_w[l], conv_b[l], dt_bias[l], a_log[l], d_skip[l],
              ssd_norm_g[l], w_branch_ssd[l], pool_w[l], pool_scale[l], w_branch_pool[l],
              gate_b[l], w_out[l])
        zc = jnp.zeros((bp, CONV_W - 1, CONV_DIM), hp.dtype)
        zp = jnp.zeros((bp, POOL_MAX - 1, D_POOL), hp.dtype)
        zh = jnp.zeros((bp, SSD_HEADS, SSD_HEAD_DIM, D_STATE), jnp.float32)
        hp, c1, p1, s1 = mixer_layer(hp, zc, zp, zh, 0, *lw)
        hs, c2, p2, s2 = mixer_layer(hs, state_conv[l], state_pool[l], state_ssd[l], PAST_LEN, *lw)
        ssd_p.append(s1); ssd_s.append(s2)
        conv_p.append(c1); conv_s.append(c2)
        pool_p.append(p1); pool_s.append(p2)
    y_prompt = rmsnorm(hp, final_g)
    y_sample = rmsnorm(hs, final_g)
    return (y_prompt, y_sample, jnp.stack(ssd_p), jnp.stack(ssd_s), jnp.stack(conv_p),
            jnp.stack(conv_s), jnp.stack(pool_p), jnp.stack(pool_s))
```

```python
import functools

import jax
import jax.numpy as jnp
from jax import lax
from jax.experimental import pallas as pl
from jax.experimental.pallas import tpu as pltpu

D_MODEL = 1024
D_INNER = 2048
SSD_HEADS = 32
SSD_HEAD_DIM = 64
SSD_GROUPS = 8
SSD_HPG = SSD_HEADS // SSD_GROUPS
D_STATE = 128
GROUP_W = SSD_HPG * SSD_HEAD_DIM
CONV_W = 4
CONV_DIM = D_INNER + 2 * SSD_GROUPS * D_STATE
D_POOL = 1024
POOL_WINDOWS = (2, 4, 8, 16)
POOL_GDIM = D_POOL // len(POOL_WINDOWS)
POOL_MAX = max(POOL_WINDOWS)
PAST_LEN = 1024
EPS = 1e-6

LANES = 128
SUBLANES = 8
CONV_HIST = SUBLANES
POOL_HIST = 2 * SUBLANES
DT_PAD = LANES
SSD_CHUNK = 128
COL_BLOCK = 512
VMEM_LIMIT_BYTES = 60 * 1024 * 1024

F32 = jnp.float32
BF16 = jnp.bfloat16


def _mm(a, b):
    return jnp.dot(a, b, preferred_element_type=F32)


def _silu(v):
    return v * jax.nn.sigmoid(v)


def _softplus(v):
    return jnp.maximum(v, 0.0) + jnp.log1p(jnp.exp(-jnp.abs(v)))


def _cumsum_rows(a):
    q = a.shape[0]
    row = lax.broadcasted_iota(jnp.int32, a.shape, 0)
    s = 1
    while s < q:
        shifted = pltpu.roll(a, s, axis=0)
        a = a + jnp.where(row >= s, shifted, 0.0)
        s *= 2
    return a


def _layer_kernel(
    x_ref, h0_ref, conv0_ref, pool0_ref,
    norm_g_ref, w_z_ref, w_xbc_ref, w_dt_ref, w_u_ref, w_pg_ref, w_gl_ref,
    conv_w_ref, conv_b_ref, dt_bias_ref, a_log_ref, d_skip_ref, ssd_norm_g_ref,
    w_bs_ref, pool_w_ref, pool_scale_ref, w_bp_ref, gate_b_ref, w_out_ref, final_g_ref,
    y_ref, hn_ref, convn_ref, pooln_ref,
    xbc_s, u_s, state_s, xs_s, b_s, c_s, yssd_s, gn_s,
    *, nb, tbs, t_valid, pos0):
    j = pl.program_id(1)
    n_t = pl.num_programs(1)
    m = nb * tbs
    q = min(SSD_CHUNK, tbs)
    n_chunks = tbs // q

    @pl.when(j == 0)
    def _():
        for b in range(nb):
            xbc_s[b, CONV_HIST - (CONV_W - 1):CONV_HIST, :] = conv0_ref[b]
            u_s[b, POOL_HIST - (POOL_MAX - 1):POOL_HIST, :] = pool0_ref[b]
            for g in range(SSD_GROUPS):
                rows = jnp.concatenate(
                    [h0_ref[b, SSD_HPG * g + i] for i in range(SSD_HPG)], axis=0)
                state_s[b, g] = rows.T

    x = x_ref[...].reshape(m, D_MODEL)
    ms = jnp.mean(x * x, axis=-1, keepdims=True)
    nrm = (x * lax.rsqrt(ms + EPS) * norm_g_ref[...]).astype(BF16)

    for cb in range(CONV_DIM // COL_BLOCK):
        cols = slice(cb * COL_BLOCK, (cb + 1) * COL_BLOCK)
        blk = _mm(nrm, w_xbc_ref[:, cols])
        for b in range(nb):
            xbc_s[b, CONV_HIST:CONV_HIST + tbs, cols] = blk[b * tbs:(b + 1) * tbs]
    for cb in range(CONV_DIM // COL_BLOCK):
        cols = slice(cb * COL_BLOCK, (cb + 1) * COL_BLOCK)
        for b in range(nb):
            rows = slice(b * tbs, (b + 1) * tbs)
            acc = conv_b_ref[:, cols]
            for k in range(CONV_W):
                lo = CONV_HIST - (CONV_W - 1) + k
                acc = acc + xbc_s[b, lo:lo + tbs, cols] * conv_w_ref[k:k + 1, cols]
            act = _silu(acc)
            c0 = cb * COL_BLOCK
            if c0 < D_INNER:
                xs_s[rows, c0:c0 + COL_BLOCK] = act
            elif c0 < D_INNER + SSD_GROUPS * D_STATE:
                b_s[rows, c0 - D_INNER:c0 - D_INNER + COL_BLOCK] = act
            else:
                o = c0 - D_INNER - SSD_GROUPS * D_STATE
                c_s[rows, o:o + COL_BLOCK] = act.astype(BF16)
    for b in range(nb):
        tail = xbc_s[b, CONV_HIST + t_valid - (CONV_W - 1):CONV_HIST + t_valid, :]
        xbc_s[b, CONV_HIST - (CONV_W - 1):CONV_HIST, :] = tail
        convn_ref[b] = tail

    dt_all = _softplus(_mm(nrm, w_dt_ref[...]) + dt_bias_ref[...])
    if t_valid < tbs:
        trow = lax.broadcasted_iota(jnp.int32, (tbs, DT_PAD), 0)
    a_neg = -jnp.exp(a_log_ref[...])
    tril = (lax.broadcasted_iota(jnp.int32, (q, q), 0)
            >= lax.broadcasted_iota(jnp.int32, (q, q), 1))
    lane_w = lax.broadcasted_iota(jnp.int32, (q, GROUP_W), 1)
    lane_half = lax.broadcasted_iota(jnp.int32, (q, LANES), 1) < SSD_HEAD_DIM
    for b in range(nb):
        for c in range(n_chunks):
            r0 = b * tbs + c * q
            rows = slice(r0, r0 + q)
            dt = dt_all[rows]
            if t_valid < tbs:
                dt = jnp.where(trow[c * q:(c + 1) * q] < t_valid, dt, 0.0)
            acum = _cumsum_rows(dt * a_neg)
            last = acum[q - 1:q, :]
            acum_t = acum.T
            dt_t = dt.T
            w_t = (dt * jnp.exp(last - acum)).T
            for g in range(SSD_GROUPS):
                gcols = slice(g * GROUP_W, (g + 1) * GROUP_W)
                ncols = slice(g * D_STATE, (g + 1) * D_STATE)
                c_g = c_s[rows, ncols]
                bt_g = b_s[rows, ncols].T
                cb_g = _mm(c_g, bt_g.astype(BF16))
                x_g = xs_s[rows, gcols]
                acc = None
                e_parts = []
                for pair in range(SSD_HPG // 2):
                    lhs_parts, rhs_parts, cols_pair = [], [], []
                    for i in (2 * pair, 2 * pair + 1):
                        h = SSD_HPG * g + i
                        col = jnp.broadcast_to(acum[:, h:h + 1], (q, q))
                        diff = col - acum_t[h:h + 1, :]
                        decay = jnp.exp(jnp.where(tril, diff, -jnp.inf))
                        m_h = (cb_g * decay * dt_t[h:h + 1, :]).astype(BF16)
                        bw_h = (bt_g * w_t[h:h + 1, :]).astype(BF16)
                        lhs_parts.append(jnp.concatenate([m_h, bw_h], axis=0))
                        in_head = (lane_w >= i * SSD_HEAD_DIM) & (lane_w < (i + 1) * SSD_HEAD_DIM)
                        rhs_parts.append(jnp.where(in_head, x_g, 0.0).astype(BF16))
                        cols_pair.append(jnp.broadcast_to(acum[:, h:h + 1], (q, LANES)))
                    part = _mm(jnp.concatenate(lhs_parts, axis=1),
                               jnp.concatenate(rhs_parts, axis=0))
                    acc = part if acc is None else acc + part
                    e_parts.append(jnp.where(lane_half, cols_pair[0], cols_pair[1]))
                e_g = jnp.exp(jnp.concatenate(e_parts, axis=1))
                st = state_s[b, g]
                y_off = _mm(c_g, st.astype(BF16)) * e_g
                y_g = acc[:q] + y_off + d_skip_ref[:, gcols] * x_g
                yssd_s[rows, gcols] = y_g
                state_s[b, g] = st * e_g[q - 1:q, :] + acc[q:]

    for g in range(SSD_GROUPS):
        gcols = slice(g * GROUP_W, (g + 1) * GROUP_W)
        z_g = _mm(nrm, w_z_ref[:, gcols])
        yf = yssd_s[:, gcols] * _silu(z_g)
        gms = jnp.mean(yf * yf, axis=-1, keepdims=True)
        gn_s[:, gcols] = (yf * lax.rsqrt(gms + EPS) * ssd_norm_g_ref[:, gcols]).astype(BF16)
    br_ssd = _mm(gn_s[...], w_bs_ref[...])

    u = _mm(nrm, w_u_ref[...])
    for b in range(nb):
        u_s[b, POOL_HIST:POOL_HIST + tbs, :] = u[b * tbs:(b + 1) * tbs]
    pos = pos0 + j * tbs + lax.broadcasted_iota(jnp.int32, (tbs, POOL_GDIM), 0)
    o_parts = []
    for gi, w in enumerate(POOL_WINDOWS):
        pcols = slice(gi * POOL_GDIM, (gi + 1) * POOL_GDIM)
        cnt = jnp.minimum(w, pos + 1).astype(F32)
        d_rows = []
        for b in range(nb):
            cur = u_s[b, POOL_HIST:POOL_HIST + tbs, pcols]
            win = cur
            for k in range(1, w):
                win = win + u_s[b, POOL_HIST - k:POOL_HIST - k + tbs, pcols]
            d_rows.append(win / cnt - cur)
        d = d_rows[0] if nb == 1 else jnp.concatenate(d_rows, axis=0)
        o_parts.append(_mm(d.astype(BF16), pool_w_ref[gi]))
    pooled = jnp.concatenate(o_parts, axis=1) * pool_scale_ref[...]
    for b in range(nb):
        tail = u_s[b, POOL_HIST + t_valid - (POOL_MAX - 1):POOL_HIST + t_valid, :]
        u_s[b, POOL_HIST - (POOL_MAX - 1):POOL_HIST, :] = tail
        pooln_ref[b] = tail
    pgate = _mm(nrm, w_pg_ref[...])
    br_pool = _mm((pooled * _silu(pgate)).astype(BF16), w_bp_ref[...])

    g_ssd = jax.nn.sigmoid(_mm(nrm, w_gl_ref[:, :D_MODEL]) + gate_b_ref[:, :D_MODEL])
    g_pool = jax.nn.sigmoid(_mm(nrm, w_gl_ref[:, D_MODEL:]) + gate_b_ref[:, D_MODEL:])
    merged = (g_ssd * br_ssd + g_pool * br_pool).astype(BF16)
    hres = x + _mm(merged, w_out_ref[...])
    hms = jnp.mean(hres * hres, axis=-1, keepdims=True)
    y_ref[...] = (hres * lax.rsqrt(hms + EPS) * final_g_ref[...]).reshape(nb, tbs, D_MODEL)

    @pl.when(j == n_t - 1)
    def _():
        for b in range(nb):
            for g in range(SSD_GROUPS):
                st_t = state_s[b, g].T
                for i in range(SSD_HPG):
                    hn_ref[b, SSD_HPG * g + i] = st_t[i * SSD_HEAD_DIM:(i + 1) * SSD_HEAD_DIM]


def _const_spec(shape):
    zeros = (0,) * len(shape)
    return pl.BlockSpec(shape, lambda i, j: zeros, pipeline_mode=pl.Buffered(1))


def _run_layer(x, h0, conv0, pool0, weights, *, nb, tbs, t_valid, pos0, name):
    n_seq, t_pad, _ = x.shape
    assert n_seq % nb == 0 and t_pad % tbs == 0
    assert t_valid == tbs or t_pad == tbs
    assert t_valid >= POOL_MAX - 1 and tbs % min(SSD_CHUNK, tbs) == 0
    m = nb * tbs
    grid = (n_seq // nb, t_pad // tbs)

    seq_spec = lambda shape: pl.BlockSpec((nb,) + shape, lambda i, j: (i,) + (0,) * len(shape))
    in_specs = [
        pl.BlockSpec((nb, tbs, D_MODEL), lambda i, j: (i, j, 0)),
        seq_spec((SSD_HEADS, SSD_HEAD_DIM, D_STATE)),
        seq_spec((CONV_W - 1, CONV_DIM)),
        seq_spec((POOL_MAX - 1, D_POOL)),
    ] + [_const_spec(w.shape) for w in weights]
    out_specs = [
        pl.BlockSpec((nb, tbs, D_MODEL), lambda i, j: (i, j, 0)),
        seq_spec((SSD_HEADS, SSD_HEAD_DIM, D_STATE)),
        seq_spec((CONV_W - 1, CONV_DIM)),
        seq_spec((POOL_MAX - 1, D_POOL)),
    ]
    out_shape = [
        jax.ShapeDtypeStruct((n_seq, t_pad, D_MODEL), F32),
        jax.ShapeDtypeStruct((n_seq, SSD_HEADS, SSD_HEAD_DIM, D_STATE), F32),
        jax.ShapeDtypeStruct((n_seq, CONV_W - 1, CONV_DIM), F32),
        jax.ShapeDtypeStruct((n_seq, POOL_MAX - 1, D_POOL), F32),
    ]
    scratch = [
        pltpu.VMEM((nb, CONV_HIST + tbs, CONV_DIM), F32),
        pltpu.VMEM((nb, POOL_HIST + tbs, D_POOL), F32),
        pltpu.VMEM((nb, SSD_GROUPS, D_STATE, GROUP_W), F32),
        pltpu.VMEM((m, D_INNER), F32),
        pltpu.VMEM((m, SSD_GROUPS * D_STATE), F32),
        pltpu.VMEM((m, SSD_GROUPS * D_STATE), BF16),
        pltpu.VMEM((m, D_INNER), F32),
        pltpu.VMEM((m, D_INNER), BF16),
    ]
    kern = functools.partial(_layer_kernel, nb=nb, tbs=tbs, t_valid=t_valid, pos0=pos0)
    return pl.pallas_call(
        kern,
        out_shape=out_shape,
        grid=grid,
        in_specs=in_specs,
        out_specs=out_specs,
        scratch_shapes=scratch,
        compiler_params=pltpu.CompilerParams(
            dimension_semantics=("arbitrary", "arbitrary"),
            vmem_limit_bytes=VMEM_LIMIT_BYTES),
        name=name,
    )(x, h0, conv0, pool0, *weights)


def _prepare_weights(norm_g, w_in, conv_w, conv_b, dt_bias, a_log, d_skip, ssd_norm_g, w_branch_ssd,
                     pool_w, pool_scale, w_branch_pool, gate_b, w_out, final_g):
    c0 = D_INNER
    c1 = c0 + CONV_DIM
    c2 = c1 + SSD_HEADS
    c3 = c2 + D_POOL
    c4 = c3 + D_POOL
    row = lambda v: v.reshape(1, -1).astype(F32)
    pad_heads = lambda v: jnp.pad(v.astype(F32), ((0, 0), (0, DT_PAD - SSD_HEADS)))
    return [
        row(norm_g),
        w_in[:, :c0].astype(BF16),
        w_in[:, c0:c1].astype(BF16),
        jnp.pad(w_in[:, c1:c2], ((0, 0), (0, DT_PAD - SSD_HEADS))).astype(BF16),
        w_in[:, c2:c3].astype(BF16),
        w_in[:, c3:c4].astype(BF16),
        w_in[:, c4:].astype(BF16),
        conv_w.astype(F32),
        row(conv_b),
        pad_heads(dt_bias.reshape(1, -1)),
        pad_heads(a_log.reshape(1, -1)),
        row(jnp.repeat(d_skip, SSD_HEAD_DIM)),
        row(ssd_norm_g),
        w_branch_ssd.astype(BF16),
        pool_w.astype(BF16),
        row(pool_scale),
        w_branch_pool.astype(BF16),
        row(gate_b),
        w_out.astype(BF16),
        row(final_g),
    ]


def kernel(x_prompt, x_sample, state_ssd, state_conv, state_pool, norm_g, w_in, conv_w, conv_b, dt_bias, a_log, d_skip, ssd_norm_g, w_branch_ssd, pool_w, pool_scale, w_branch_pool, gate_b, w_out, final_g):
    depth = w_in.shape[0]
    assert depth == 1, "single-layer step"
    weights = _prepare_weights(norm_g[0], w_in[0], conv_w[0], conv_b[0], dt_bias[0], a_log[0], d_skip[0],
                               ssd_norm_g[0], w_branch_ssd[0], pool_w[0], pool_scale[0],
                               w_branch_pool[0], gate_b[0], w_out[0], final_g)
    bp, tp, _ = x_prompt.shape
    bs, ts, _ = x_sample.shape

    yp, hp, cp, pp = _run_layer(
        x_prompt,
        jnp.zeros((bp, SSD_HEADS, SSD_HEAD_DIM, D_STATE), F32),
        jnp.zeros((bp, CONV_W - 1, CONV_DIM), F32),
        jnp.zeros((bp, POOL_MAX - 1, D_POOL), F32),
        weights, nb=1, tbs=256, t_valid=256, pos0=0, name="layer_prompt")

    ts_pad = SSD_CHUNK
    xs_pad = jnp.pad(x_sample, ((0, 0), (0, ts_pad - ts), (0, 0)))
    ys, hs, cs, ps = _run_layer(
        xs_pad, state_ssd[0], state_conv[0], state_pool[0],
        weights, nb=2, tbs=ts_pad, t_valid=ts, pos0=PAST_LEN, name="layer_sample")

    return (yp, ys[:, :ts], hp[None], hs[None], cp[None], cs[None], pp[None], ps[None])
```

```python
import functools

import jax
import jax.numpy as jnp
from jax import lax
from jax.experimental import pallas as pl
from jax.experimental.pallas import tpu as pltpu

D_MODEL = 1024
D_INNER = 2048
SSD_HEADS = 32
SSD_HEAD_DIM = 64
SSD_GROUPS = 8
SSD_HPG = SSD_HEADS // SSD_GROUPS
D_STATE = 128
GROUP_W = SSD_HPG * SSD_HEAD_DIM
CONV_W = 4
CONV_DIM = D_INNER + 2 * SSD_GROUPS * D_STATE
D_POOL = 1024
POOL_WINDOWS = (2, 4, 8, 16)
POOL_GDIM = D_POOL // len(POOL_WINDOWS)
POOL_MAX = max(POOL_WINDOWS)
PAST_LEN = 1024
EPS = 1e-6

LANES = 128
SUBLANES = 8
CONV_HIST = SUBLANES
POOL_HIST = 2 * SUBLANES
DT_PAD = LANES
SSD_CHUNK = 128
COL_BLOCK = 512
VMEM_LIMIT_BYTES = 60 * 1024 * 1024

CONV_SLABS = CONV_DIM // LANES
X_SLABS = D_INNER // LANES
B_SLAB0 = X_SLABS
C_SLAB0 = X_SLABS + SSD_GROUPS * D_STATE // LANES
POOL_SLABS = D_POOL // LANES

F32 = jnp.float32
BF16 = jnp.bfloat16


def _mm(a, b):
    return jnp.dot(a, b, preferred_element_type=F32)


def _w(ref_slice):
    return pltpu.bitcast(ref_slice, BF16)


def _silu(v):
    return v * jax.nn.sigmoid(v)


def _softplus(v):
    return jnp.maximum(v, 0.0) + jnp.log1p(jnp.exp(-jnp.abs(v)))


def _cumsum_rows(a):
    q = a.shape[0]
    row = lax.broadcasted_iota(jnp.int32, a.shape, 0)
    s = 1
    while s < q:
        shifted = pltpu.roll(a, s, axis=0)
        a = a + jnp.where(row >= s, shifted, 0.0)
        s *= 2
    return a


def _layer_kernel(
    x_ref, h0_ref, conv0_ref, pool0_ref,
    norm_g_ref, w_z_ref, w_xbc_ref, w_dt_ref, w_u_ref, w_pg_ref, w_gl_ref,
    conv_w_ref, conv_b_ref, dt_bias_ref, a_log_ref, d_skip_ref, ssd_norm_g_ref,
    w_bs_ref, pool_w_ref, pool_scale_ref, w_bp_ref, gate_b_ref, w_out_ref, final_g_ref,
    y_ref, hn_ref, convn_ref, pooln_ref,
    xbc_s, co_s, u_s, d_s, state_s, yssd_s, gn_s, z_s, pg_s, gl_s,
    *, nb, tbs, t_valid, pos0):
    j = pl.program_id(1)
    n_t = pl.num_programs(1)
    m = nb * tbs
    half = tbs // 2
    q = min(SSD_CHUNK, tbs)
    n_chunks = tbs // q
    conv_tail = CONV_HIST + t_valid - (CONV_W - 1)
    pool_tail = POOL_HIST + t_valid - (POOL_MAX - 1)

    @pl.when(j == 0)
    def _():
        for b in range(nb):
            for s in range(CONV_SLABS):
                xbc_s[b, s, CONV_HIST - (CONV_W - 1):CONV_HIST, :] = (
                    conv0_ref[b, :, s * LANES:(s + 1) * LANES])
            for s in range(POOL_SLABS):
                u_s[b, s, POOL_HIST - (POOL_MAX - 1):POOL_HIST, :] = (
                    pool0_ref[b, :, s * LANES:(s + 1) * LANES])
            for g in range(SSD_GROUPS):
                rows = jnp.concatenate(
                    [h0_ref[b, SSD_HPG * g + i] for i in range(SSD_HPG)], axis=0)
                state_s[b, g] = rows.T

    x = x_ref[...].reshape(m, D_MODEL)
    ms = jnp.mean(x * x, axis=-1, keepdims=True)
    nrm = (x * lax.rsqrt(ms + EPS) * norm_g_ref[...]).astype(BF16)

    slabs_per_block = COL_BLOCK // LANES

    def project_block(cb):
        blk = _mm(nrm, _w(w_xbc_ref[:, cb * COL_BLOCK:(cb + 1) * COL_BLOCK]))
        for i in range(slabs_per_block):
            for b in range(nb):
                xbc_s[b, cb * slabs_per_block + i, CONV_HIST:CONV_HIST + tbs, :] = (
                    blk[b * tbs:(b + 1) * tbs, i * LANES:(i + 1) * LANES])

    def conv_block(cb):
        for s in range(cb * slabs_per_block, (cb + 1) * slabs_per_block):
            lanes = slice(s * LANES, (s + 1) * LANES)
            taps = [conv_w_ref[k:k + 1, lanes] for k in range(CONV_W)]
            bias = conv_b_ref[:, lanes]
            for b in range(nb):
                for par in range(2):
                    acc = bias
                    for k in range(CONV_W):
                        lo = CONV_HIST + par - (CONV_W - 1) + k
                        acc = acc + xbc_s[b, s, pl.ds(lo, half, stride=2), :] * taps[k]
                    co_s[s, pl.ds(b * tbs + par, half, stride=2), :] = _silu(acc)

    n_blocks = CONV_DIM // COL_BLOCK
    project_block(0)
    for cb in range(1, n_blocks):
        project_block(cb)
        conv_block(cb - 1)
    conv_block(n_blocks - 1)
    for b in range(nb):
        for s in range(CONV_SLABS):
            xbc_s[b, s, CONV_HIST - (CONV_W - 1):CONV_HIST, :] = (
                xbc_s[b, s, conv_tail:conv_tail + CONV_W - 1, :])

    def store_u(c0, blk):
        for i in range(blk.shape[1] // LANES):
            for b in range(nb):
                u_s[b, c0 // LANES + i, POOL_HIST:POOL_HIST + tbs, :] = (
                    blk[b * tbs:(b + 1) * tbs, i * LANES:(i + 1) * LANES])

    def store_cols(dst):
        def store(c0, blk):
            dst[:, c0:c0 + blk.shape[1]] = blk
        return store

    def dense_task(w_ref, c0, width, store):
        def run():
            store(c0, _mm(nrm, _w(w_ref[:, c0:c0 + width])))
        return run

    dense_tasks = (
        [dense_task(w_z_ref, c, GROUP_W, store_cols(z_s)) for c in range(0, D_INNER, GROUP_W)]
        + [dense_task(w_u_ref, c, COL_BLOCK, store_u) for c in range(0, D_POOL, COL_BLOCK)]
        + [dense_task(w_pg_ref, c, COL_BLOCK, store_cols(pg_s)) for c in range(0, D_POOL, COL_BLOCK)]
        + [dense_task(w_gl_ref, c, COL_BLOCK, store_cols(gl_s)) for c in range(0, 2 * D_MODEL, COL_BLOCK)])
    dt_all = _softplus(_mm(nrm, _w(w_dt_ref[...])) + dt_bias_ref[...])
    if t_valid < tbs:
        trow = lax.broadcasted_iota(jnp.int32, (tbs, DT_PAD), 0)
    a_neg = -jnp.exp(a_log_ref[...])
    tril = (lax.broadcasted_iota(jnp.int32, (q, q), 0)
            >= lax.broadcasted_iota(jnp.int32, (q, q), 1))
    lane_w = lax.broadcasted_iota(jnp.int32, (q, GROUP_W), 1)
    lane_half = lax.broadcasted_iota(jnp.int32, (q, LANES), 1) < SSD_HEAD_DIM

    def chunk_rows(b, c):
        r0 = b * tbs + c * q
        return slice(r0, r0 + q)

    def chunk_terms(b, c):
        dt = dt_all[chunk_rows(b, c)]
        if t_valid < tbs:
            dt = jnp.where(trow[c * q:(c + 1) * q] < t_valid, dt, 0.0)
        acum = _cumsum_rows(dt * a_neg)
        last = acum[q - 1:q, :]
        return dict(acum=acum, acum_t=acum.T, dt_t=dt.T,
                    w_t=(dt * jnp.exp(last - acum)).T)

    def group_begin(b, c, g):
        rows = chunk_rows(b, c)
        c_g = co_s[C_SLAB0 + g, rows, :].astype(BF16)
        bt_g = co_s[B_SLAB0 + g, rows, :].T
        x_g = jnp.concatenate(
            [co_s[2 * g, rows, :], co_s[2 * g + 1, rows, :]], axis=1)
        return dict(c_g=c_g, bt_g=bt_g, x_g=x_g, cb_g=_mm(c_g, bt_g.astype(BF16)))

    def group_finish(b, c, g, ops, terms):
        rows = chunk_rows(b, c)
        gcols = slice(g * GROUP_W, (g + 1) * GROUP_W)
        c_g, bt_g, x_g, cb_g = ops["c_g"], ops["bt_g"], ops["x_g"], ops["cb_g"]
        acum, acum_t, dt_t, w_t = terms["acum"], terms["acum_t"], terms["dt_t"], terms["w_t"]
        acc = None
        e_parts = []
        for pair in range(SSD_HPG // 2):
            lhs_parts, rhs_parts, cols_pair = [], [], []
            for i in (2 * pair, 2 * pair + 1):
                h = SSD_HPG * g + i
                col = jnp.broadcast_to(acum[:, h:h + 1], (q, q))
                diff = col - acum_t[h:h + 1, :]
                decay = jnp.exp(jnp.where(tril, diff, -jnp.inf))
                m_h = (cb_g * decay * dt_t[h:h + 1, :]).astype(BF16)
                bw_h = (bt_g * w_t[h:h + 1, :]).astype(BF16)
                lhs_parts.append(jnp.concatenate([m_h, bw_h], axis=0))
                in_head = (lane_w >= i * SSD_HEAD_DIM) & (lane_w < (i + 1) * SSD_HEAD_DIM)
                rhs_parts.append(jnp.where(in_head, x_g, 0.0).astype(BF16))
                cols_pair.append(jnp.broadcast_to(acum[:, h:h + 1], (q, LANES)))
            part = _mm(jnp.concatenate(lhs_parts, axis=1),
                       jnp.concatenate(rhs_parts, axis=0))
            acc = part if acc is None else acc + part
            e_parts.append(jnp.where(lane_half, cols_pair[0], cols_pair[1]))
        e_g = jnp.exp(jnp.concatenate(e_parts, axis=1))
        st = state_s[b, g]
        y_off = _mm(c_g, st.astype(BF16)) * e_g
        yssd_s[rows, gcols] = acc[:q] + y_off + d_skip_ref[:, gcols] * x_g
        state_s[b, g] = st * e_g[q - 1:q, :] + acc[q:]

    slots = [(b, c, g) for b in range(nb) for c in range(n_chunks) for g in range(SSD_GROUPS)]
    terms = {(b, c): chunk_terms(b, c) for b in range(nb) for c in range(n_chunks)}
    done = 0
    ahead = group_begin(*slots[0])
    for k, (b, c, g) in enumerate(slots):
        ops = ahead
        if k + 1 < len(slots):
            ahead = group_begin(*slots[k + 1])
        while done * len(slots) < (k + 1) * len(dense_tasks):
            dense_tasks[done]()
            done += 1
        group_finish(b, c, g, ops, terms[(b, c)])

    for g in range(SSD_GROUPS):
        gcols = slice(g * GROUP_W, (g + 1) * GROUP_W)
        yf = yssd_s[:, gcols] * _silu(z_s[:, gcols])
        gms = jnp.mean(yf * yf, axis=-1, keepdims=True)
        gn_s[:, gcols] = (yf * lax.rsqrt(gms + EPS) * ssd_norm_g_ref[:, gcols]).astype(BF16)
    br_ssd = _mm(gn_s[...], _w(w_bs_ref[...]))

    pos_half = pos0 + j * tbs + 2 * lax.broadcasted_iota(jnp.int32, (half, LANES), 0)
    slabs_per_group = POOL_GDIM // LANES
    for gi, w in enumerate(POOL_WINDOWS):
        for par in range(2):
            cnt = jnp.minimum(w, pos_half + (par + 1)).astype(F32)
            for s in range(gi * slabs_per_group, (gi + 1) * slabs_per_group):
                for b in range(nb):
                    cur = u_s[b, s, pl.ds(POOL_HIST + par, half, stride=2), :]
                    win = cur
                    for k in range(1, w):
                        win = win + u_s[b, s, pl.ds(POOL_HIST + par - k, half, stride=2), :]
                    d_s[s, pl.ds(b * tbs + par, half, stride=2), :] = win / cnt - cur
    o_parts = []
    for gi in range(len(POOL_WINDOWS)):
        d = jnp.concatenate(
            [d_s[s] for s in range(gi * slabs_per_group, (gi + 1) * slabs_per_group)], axis=1)
        o_parts.append(_mm(d.astype(BF16), _w(pool_w_ref[gi])))
    pooled = jnp.concatenate(o_parts, axis=1) * pool_scale_ref[...]
    for b in range(nb):
        for s in range(POOL_SLABS):
            u_s[b, s, POOL_HIST - (POOL_MAX - 1):POOL_HIST, :] = (
                u_s[b, s, pool_tail:pool_tail + POOL_MAX - 1, :])
    br_pool = _mm((pooled * _silu(pg_s[...])).astype(BF16), _w(w_bp_ref[...]))

    g_ssd = jax.nn.sigmoid(gl_s[:, :D_MODEL] + gate_b_ref[:, :D_MODEL])
    g_pool = jax.nn.sigmoid(gl_s[:, D_MODEL:] + gate_b_ref[:, D_MODEL:])
    merged = (g_ssd * br_ssd + g_pool * br_pool).astype(BF16)
    hres = x + _mm(merged, _w(w_out_ref[...]))
    hms = jnp.mean(hres * hres, axis=-1, keepdims=True)
    y_ref[...] = (hres * lax.rsqrt(hms + EPS) * final_g_ref[...]).reshape(nb, tbs, D_MODEL)

    @pl.when(j == n_t - 1)
    def _():
        for b in range(nb):
            for s in range(CONV_SLABS):
                convn_ref[b, :, s * LANES:(s + 1) * LANES] = (
                    xbc_s[b, s, CONV_HIST - (CONV_W - 1):CONV_HIST, :])
            for s in range(POOL_SLABS):
                pooln_ref[b, :, s * LANES:(s + 1) * LANES] = (
                    u_s[b, s, POOL_HIST - (POOL_MAX - 1):POOL_HIST, :])
            for g in range(SSD_GROUPS):
                st_t = state_s[b, g].T
                for i in range(SSD_HPG):
                    hn_ref[b, SSD_HPG * g + i] = st_t[i * SSD_HEAD_DIM:(i + 1) * SSD_HEAD_DIM]


def _const_spec(shape):
    zeros = (0,) * len(shape)
    return pl.BlockSpec(shape, lambda i, j: zeros, pipeline_mode=pl.Buffered(1))


def _run_layer(x, h0, conv0, pool0, weights, *, nb, tbs, t_valid, pos0, name):
    n_seq, t_pad, _ = x.shape
    assert n_seq % nb == 0 and t_pad % tbs == 0 and tbs % (2 * SUBLANES) == 0
    assert t_valid == tbs or t_pad == tbs
    assert t_valid >= POOL_MAX - 1 and tbs % min(SSD_CHUNK, tbs) == 0
    m = nb * tbs
    grid = (n_seq // nb, t_pad // tbs)

    seq_spec = lambda shape, **kw: pl.BlockSpec(
        (nb,) + shape, lambda i, j: (i,) + (0,) * len(shape), **kw)
    once = dict(pipeline_mode=pl.Buffered(1))
    in_specs = [
        pl.BlockSpec((nb, tbs, D_MODEL), lambda i, j: (i, j, 0)),
        seq_spec((SSD_HEADS, SSD_HEAD_DIM, D_STATE), **once),
        seq_spec((CONV_W - 1, CONV_DIM), **once),
        seq_spec((POOL_MAX - 1, D_POOL), **once),
    ] + [_const_spec(w.shape) for w in weights]
    out_specs = [
        pl.BlockSpec((nb, tbs, D_MODEL), lambda i, j: (i, j, 0)),
        seq_spec((SSD_HEADS, SSD_HEAD_DIM, D_STATE)),
        seq_spec((CONV_W - 1, CONV_DIM)),
        seq_spec((POOL_MAX - 1, D_POOL)),
    ]
    out_shape = [
        jax.ShapeDtypeStruct((n_seq, t_pad, D_MODEL), F32),
        jax.ShapeDtypeStruct((n_seq, SSD_HEADS, SSD_HEAD_DIM, D_STATE), F32),
        jax.ShapeDtypeStruct((n_seq, CONV_W - 1, CONV_DIM), F32),
        jax.ShapeDtypeStruct((n_seq, POOL_MAX - 1, D_POOL), F32),
    ]
    scratch = [
        pltpu.VMEM((nb, CONV_SLABS, CONV_HIST + tbs, LANES), F32),
        pltpu.VMEM((CONV_SLABS, m, LANES), F32),
        pltpu.VMEM((nb, POOL_SLABS, POOL_HIST + tbs, LANES), F32),
        pltpu.VMEM((POOL_SLABS, m, LANES), F32),
        pltpu.VMEM((nb, SSD_GROUPS, D_STATE, GROUP_W), F32),
        pltpu.VMEM((m, D_INNER), F32),
        pltpu.VMEM((m, D_INNER), BF16),
        pltpu.VMEM((m, D_INNER), F32),
        pltpu.VMEM((m, D_POOL), F32),
        pltpu.VMEM((m, 2 * D_MODEL), F32),
    ]
    kern = functools.partial(_layer_kernel, nb=nb, tbs=tbs, t_valid=t_valid, pos0=pos0)
    return pl.pallas_call(
        kern,
        out_shape=out_shape,
        grid=grid,
        in_specs=in_specs,
        out_specs=out_specs,
        scratch_shapes=scratch,
        compiler_params=pltpu.CompilerParams(
            dimension_semantics=("arbitrary", "arbitrary"),
            vmem_limit_bytes=VMEM_LIMIT_BYTES),
        name=name,
    )(x, h0, conv0, pool0, *weights)


def _pack_rows(w):
    *lead, k, n = w.shape
    pairs = jnp.swapaxes(w.astype(BF16).reshape(*lead, k // 2, 2, n), -1, -2)
    return lax.bitcast_convert_type(pairs, jnp.uint32)


def _prepare_weights(norm_g, w_in, conv_w, conv_b, dt_bias, a_log, d_skip, ssd_norm_g, w_branch_ssd,
                     pool_w, pool_scale, w_branch_pool, gate_b, w_out, final_g):
    c0 = D_INNER
    c1 = c0 + CONV_DIM
    c2 = c1 + SSD_HEADS
    c3 = c2 + D_POOL
    c4 = c3 + D_POOL
    row = lambda v: v.reshape(1, -1).astype(F32)
    pad_heads = lambda v: jnp.pad(v.astype(F32), ((0, 0), (0, DT_PAD - SSD_HEADS)))
    return [
        row(norm_g),
        _pack_rows(w_in[:, :c0]),
        _pack_rows(w_in[:, c0:c1]),
        _pack_rows(jnp.pad(w_in[:, c1:c2], ((0, 0), (0, DT_PAD - SSD_HEADS)))),
        _pack_rows(w_in[:, c2:c3]),
        _pack_rows(w_in[:, c3:c4]),
        _pack_rows(w_in[:, c4:]),
        conv_w.astype(F32),
        row(conv_b),
        pad_heads(dt_bias.reshape(1, -1)),
        pad_heads(a_log.reshape(1, -1)),
        row(jnp.repeat(d_skip, SSD_HEAD_DIM)),
        row(ssd_norm_g),
        _pack_rows(w_branch_ssd),
        _pack_rows(pool_w),
        row(pool_scale),
        _pack_rows(w_branch_pool),
        row(gate_b),
        _pack_rows(w_out),
        row(final_g),
    ]


def kernel(x_prompt, x_sample, state_ssd, state_conv, state_pool, norm_g, w_in, conv_w, conv_b, dt_bias, a_log, d_skip, ssd_norm_g, w_branch_ssd, pool_w, pool_scale, w_branch_pool, gate_b, w_out, final_g):
    depth = w_in.shape[0]
    assert depth == 1, "single-layer step"
    weights = _prepare_weights(norm_g[0], w_in[0], conv_w[0], conv_b[0], dt_bias[0], a_log[0], d_skip[0],
                               ssd_norm_g[0], w_branch_ssd[0], pool_w[0], pool_scale[0],
                               w_branch_pool[0], gate_b[0], w_out[0], final_g)
    bp, tp, _ = x_prompt.shape
    bs, ts, _ = x_sample.shape

    yp, hp, cp, pp = _run_layer(
        x_prompt,
        jnp.zeros((bp, SSD_HEADS, SSD_HEAD_DIM, D_STATE), F32),
        jnp.zeros((bp, CONV_W - 1, CONV_DIM), F32),
        jnp.zeros((bp, POOL_MAX - 1, D_POOL), F32),
        weights, nb=1, tbs=256, t_valid=256, pos0=0, name="layer_prompt")

    ts_pad = SSD_CHUNK
    xs_pad = jnp.pad(x_sample, ((0, 0), (0, ts_pad - ts), (0, 0)))
    ys, hs, cs, ps = _run_layer(
        xs_pad, state_ssd[0], state_conv[0], state_pool[0],
        weights, nb=1, tbs=ts_pad, t_valid=ts, pos0=PAST_LEN, name="layer_sample")

    return (yp, ys[:, :ts], hp[None], hs[None], cp[None], cs[None], pp[None], ps[None])
```

```python
import functools

import jax
import jax.numpy as jnp
from jax import lax
from jax.experimental import pallas as pl
from jax.experimental.pallas import tpu as pltpu

D_MODEL = 1024
D_INNER = 2048
SSD_HEADS = 32
SSD_HEAD_DIM = 64
SSD_GROUPS = 8
SSD_HPG = SSD_HEADS // SSD_GROUPS
D_STATE = 128
GROUP_W = SSD_HPG * SSD_HEAD_DIM
CONV_W = 4
CONV_DIM = D_INNER + 2 * SSD_GROUPS * D_STATE
D_POOL = 1024
POOL_WINDOWS = (2, 4, 8, 16)
POOL_GDIM = D_POOL // len(POOL_WINDOWS)
POOL_MAX = max(POOL_WINDOWS)
PAST_LEN = 1024
EPS = 1e-6

LANES = 128
SUBLANES = 8
CONV_HIST = SUBLANES
POOL_HIST = 2 * SUBLANES
DT_PAD = LANES
SSD_CHUNK = 128
COL_BLOCK = 512
VMEM_LIMIT_BYTES = 60 * 1024 * 1024

CONV_SLABS = CONV_DIM // LANES
X_SLABS = D_INNER // LANES
B_SLAB0 = X_SLABS
C_SLAB0 = X_SLABS + SSD_GROUPS * D_STATE // LANES
POOL_SLABS = D_POOL // LANES

ZX_Z0, ZX_XBC0, ZX_COLS = 0, D_INNER, D_INNER + CONV_DIM
IN_DT0 = ZX_COLS
IN_UPG0 = IN_DT0 + SSD_HEADS
UPG_U0, UPG_PG0, UPG_GL0, UPG_COLS = 0, D_POOL, 2 * D_POOL, 2 * D_POOL + 2 * D_MODEL
PACK_COLS = 512

F32 = jnp.float32
BF16 = jnp.bfloat16


def _mm(a, b):
    return jnp.dot(a, b, preferred_element_type=F32)


def _w(ref_slice):
    return pltpu.bitcast(ref_slice, BF16)


def _silu(v):
    return v * jax.nn.sigmoid(v)


def _softplus(v):
    return jnp.maximum(v, 0.0) + jnp.log1p(jnp.exp(-jnp.abs(v)))


def _cumsum_rows(a):
    q = a.shape[0]
    row = lax.broadcasted_iota(jnp.int32, a.shape, 0)
    s = 1
    while s < q:
        shifted = pltpu.roll(a, s, axis=0)
        a = a + jnp.where(row >= s, shifted, 0.0)
        s *= 2
    return a


def _layer_kernel(
    x_ref, h0_ref, conv0_ref, pool0_ref,
    norm_g_ref, w_zx_ref, w_dt_ref, w_upg_ref,
    conv_w_ref, conv_b_ref, dt_bias_ref, a_log_ref, d_skip_ref, ssd_norm_g_ref,
    w_bs_ref, pool_w_ref, pool_scale_ref, w_bp_ref, gate_b_ref, w_out_ref, final_g_ref,
    y_ref, hn_ref, convn_ref, pooln_ref,
    xbc_s, co_s, u_s, d_s, state_s, yssd_s, gn_s, z_s, pg_s, gl_s,
    *, nb, tbs, t_valid, pos0):
    j = pl.program_id(1)
    n_t = pl.num_programs(1)
    m = nb * tbs
    half = tbs // 2
    q = min(SSD_CHUNK, tbs)
    n_chunks = tbs // q
    conv_tail = CONV_HIST + t_valid - (CONV_W - 1)
    pool_tail = POOL_HIST + t_valid - (POOL_MAX - 1)

    @pl.when(j == 0)
    def _():
        for b in range(nb):
            for s in range(CONV_SLABS):
                xbc_s[b, s, CONV_HIST - (CONV_W - 1):CONV_HIST, :] = (
                    conv0_ref[b, :, s * LANES:(s + 1) * LANES])
            for s in range(POOL_SLABS):
                u_s[b, s, POOL_HIST - (POOL_MAX - 1):POOL_HIST, :] = (
                    pool0_ref[b, :, s * LANES:(s + 1) * LANES])
            for g in range(SSD_GROUPS):
                rows = jnp.concatenate(
                    [h0_ref[b, SSD_HPG * g + i] for i in range(SSD_HPG)], axis=0)
                state_s[b, g] = rows.T

    x = x_ref[...].reshape(m, D_MODEL)
    ms = jnp.mean(x * x, axis=-1, keepdims=True)
    nrm = (x * lax.rsqrt(ms + EPS) * norm_g_ref[...]).astype(BF16)

    slabs_per_block = COL_BLOCK // LANES

    def project_block(cb):
        c0 = ZX_XBC0 + cb * COL_BLOCK
        blk = _mm(nrm, _w(w_zx_ref[:, c0:c0 + COL_BLOCK]))
        for i in range(slabs_per_block):
            for b in range(nb):
                xbc_s[b, cb * slabs_per_block + i, CONV_HIST:CONV_HIST + tbs, :] = (
                    blk[b * tbs:(b + 1) * tbs, i * LANES:(i + 1) * LANES])

    def conv_block(cb):
        for s in range(cb * slabs_per_block, (cb + 1) * slabs_per_block):
            lanes = slice(s * LANES, (s + 1) * LANES)
            taps = [conv_w_ref[k:k + 1, lanes] for k in range(CONV_W)]
            bias = conv_b_ref[:, lanes]
            for b in range(nb):
                for par in range(2):
                    acc = bias
                    for k in range(CONV_W):
                        lo = CONV_HIST + par - (CONV_W - 1) + k
                        acc = acc + xbc_s[b, s, pl.ds(lo, half, stride=2), :] * taps[k]
                    co_s[s, pl.ds(b * tbs + par, half, stride=2), :] = _silu(acc)

    n_blocks = CONV_DIM // COL_BLOCK
    project_block(0)
    for cb in range(1, n_blocks):
        project_block(cb)
        conv_block(cb - 1)
    conv_block(n_blocks - 1)
    for b in range(nb):
        for s in range(CONV_SLABS):
            xbc_s[b, s, CONV_HIST - (CONV_W - 1):CONV_HIST, :] = (
                xbc_s[b, s, conv_tail:conv_tail + CONV_W - 1, :])

    def store_u(c0, blk):
        for i in range(blk.shape[1] // LANES):
            for b in range(nb):
                u_s[b, c0 // LANES + i, POOL_HIST:POOL_HIST + tbs, :] = (
                    blk[b * tbs:(b + 1) * tbs, i * LANES:(i + 1) * LANES])

    def store_cols(dst):
        def store(c0, blk):
            dst[:, c0:c0 + blk.shape[1]] = blk
        return store

    def dense_task(w_ref, w_c0, c0, width, store):
        def run():
            store(c0, _mm(nrm, _w(w_ref[:, w_c0 + c0:w_c0 + c0 + width])))
        return run

    dense_tasks = (
        [dense_task(w_zx_ref, ZX_Z0, c, GROUP_W, store_cols(z_s)) for c in range(0, D_INNER, GROUP_W)]
        + [dense_task(w_upg_ref, UPG_U0, c, COL_BLOCK, store_u) for c in range(0, D_POOL, COL_BLOCK)]
        + [dense_task(w_upg_ref, UPG_PG0, c, COL_BLOCK, store_cols(pg_s))
           for c in range(0, D_POOL, COL_BLOCK)]
        + [dense_task(w_upg_ref, UPG_GL0, c, COL_BLOCK, store_cols(gl_s))
           for c in range(0, 2 * D_MODEL, COL_BLOCK)])
    dt_all = _softplus(_mm(nrm, _w(w_dt_ref[...])) + dt_bias_ref[...])
    if t_valid < tbs:
        trow = lax.broadcasted_iota(jnp.int32, (tbs, DT_PAD), 0)
    a_neg = -jnp.exp(a_log_ref[...])
    tril = (lax.broadcasted_iota(jnp.int32, (q, q), 0)
            >= lax.broadcasted_iota(jnp.int32, (q, q), 1))
    lane_w = lax.broadcasted_iota(jnp.int32, (q, GROUP_W), 1)
    lane_half = lax.broadcasted_iota(jnp.int32, (q, LANES), 1) < SSD_HEAD_DIM

    def chunk_rows(b, c):
        r0 = b * tbs + c * q
        return slice(r0, r0 + q)

    def chunk_terms(b, c):
        dt = dt_all[chunk_rows(b, c)]
        if t_valid < tbs:
            dt = jnp.where(trow[c * q:(c + 1) * q] < t_valid, dt, 0.0)
        acum = _cumsum_rows(dt * a_neg)
        last = acum[q - 1:q, :]
        return dict(acum=acum, acum_t=acum.T, dt_t=dt.T,
                    w_t=(dt * jnp.exp(last - acum)).T)

    def group_begin(b, c, g):
        rows = chunk_rows(b, c)
        c_g = co_s[C_SLAB0 + g, rows, :].astype(BF16)
        bt_g = co_s[B_SLAB0 + g, rows, :].T
        x_g = jnp.concatenate(
            [co_s[2 * g, rows, :], co_s[2 * g + 1, rows, :]], axis=1)
        return dict(c_g=c_g, bt_g=bt_g, x_g=x_g, cb_g=_mm(c_g, bt_g.astype(BF16)))

    def group_finish(b, c, g, ops, terms):
        rows = chunk_rows(b, c)
        gcols = slice(g * GROUP_W, (g + 1) * GROUP_W)
        c_g, bt_g, x_g, cb_g = ops["c_g"], ops["bt_g"], ops["x_g"], ops["cb_g"]
        acum, acum_t, dt_t, w_t = terms["acum"], terms["acum_t"], terms["dt_t"], terms["w_t"]
        acc = None
        e_parts = []
        for pair in range(SSD_HPG // 2):
            lhs_parts, rhs_parts, cols_pair = [], [], []
            for i in (2 * pair, 2 * pair + 1):
                h = SSD_HPG * g + i
                col = jnp.broadcast_to(acum[:, h:h + 1], (q, q))
                diff = col - acum_t[h:h + 1, :]
                decay = jnp.exp(jnp.where(tril, diff, -jnp.inf))
                m_h = (cb_g * decay * dt_t[h:h + 1, :]).astype(BF16)
                bw_h = (bt_g * w_t[h:h + 1, :]).astype(BF16)
                lhs_parts.append(jnp.concatenate([m_h, bw_h], axis=0))
                in_head = (lane_w >= i * SSD_HEAD_DIM) & (lane_w < (i + 1) * SSD_HEAD_DIM)
                rhs_parts.append(jnp.where(in_head, x_g, 0.0).astype(BF16))
                cols_pair.append(jnp.broadcast_to(acum[:, h:h + 1], (q, LANES)))
            part = _mm(jnp.concatenate(lhs_parts, axis=1),
                       jnp.concatenate(rhs_parts, axis=0))
            acc = part if acc is None else acc + part
            e_parts.append(jnp.where(lane_half, cols_pair[0], cols_pair[1]))
        e_g = jnp.exp(jnp.concatenate(e_parts, axis=1))
        st = state_s[b, g]
        y_off = _mm(c_g, st.astype(BF16)) * e_g
        yssd_s[rows, gcols] = acc[:q] + y_off + d_skip_ref[:, gcols] * x_g
        state_s[b, g] = st * e_g[q - 1:q, :] + acc[q:]

    slots = [(b, c, g) for b in range(nb) for c in range(n_chunks) for g in range(SSD_GROUPS)]
    terms = {(b, c): chunk_terms(b, c) for b in range(nb) for c in range(n_chunks)}
    done = 0
    ahead = group_begin(*slots[0])
    for k, (b, c, g) in enumerate(slots):
        ops = ahead
        if k + 1 < len(slots):
            ahead = group_begin(*slots[k + 1])
        while done * len(slots) < (k + 1) * len(dense_tasks):
            dense_tasks[done]()
            done += 1
        group_finish(b, c, g, ops, terms[(b, c)])

    for g in range(SSD_GROUPS):
        gcols = slice(g * GROUP_W, (g + 1) * GROUP_W)
        yf = yssd_s[:, gcols] * _silu(z_s[:, gcols])
        gms = jnp.mean(yf * yf, axis=-1, keepdims=True)
        gn_s[:, gcols] = (yf * lax.rsqrt(gms + EPS) * ssd_norm_g_ref[:, gcols]).astype(BF16)
    br_ssd = _mm(gn_s[...], _w(w_bs_ref[...]))

    pos_half = pos0 + j * tbs + 2 * lax.broadcasted_iota(jnp.int32, (half, LANES), 0)
    slabs_per_group = POOL_GDIM // LANES
    for gi, w in enumerate(POOL_WINDOWS):
        for par in range(2):
            cnt = jnp.minimum(w, pos_half + (par + 1)).astype(F32)
            for s in range(gi * slabs_per_group, (gi + 1) * slabs_per_group):
                for b in range(nb):
                    cur = u_s[b, s, pl.ds(POOL_HIST + par, half, stride=2), :]
                    win = cur
                    for k in range(1, w):
                        win = win + u_s[b, s, pl.ds(POOL_HIST + par - k, half, stride=2), :]
                    d_s[s, pl.ds(b * tbs + par, half, stride=2), :] = win / cnt - cur
    o_parts = []
    for gi in range(len(POOL_WINDOWS)):
        d = jnp.concatenate(
            [d_s[s] for s in range(gi * slabs_per_group, (gi + 1) * slabs_per_group)], axis=1)
        o_parts.append(_mm(d.astype(BF16), _w(pool_w_ref[gi * (POOL_GDIM // 2):(gi + 1) * (POOL_GDIM // 2), :])))
    pooled = jnp.concatenate(o_parts, axis=1) * pool_scale_ref[...]
    for b in range(nb):
        for s in range(POOL_SLABS):
            u_s[b, s, POOL_HIST - (POOL_MAX - 1):POOL_HIST, :] = (
                u_s[b, s, pool_tail:pool_tail + POOL_MAX - 1, :])
    br_pool = _mm((pooled * _silu(pg_s[...])).astype(BF16), _w(w_bp_ref[...]))

    g_ssd = jax.nn.sigmoid(gl_s[:, :D_MODEL] + gate_b_ref[:, :D_MODEL])
    g_pool = jax.nn.sigmoid(gl_s[:, D_MODEL:] + gate_b_ref[:, D_MODEL:])
    merged = (g_ssd * br_ssd + g_pool * br_pool).astype(BF16)
    hres = x + _mm(merged, _w(w_out_ref[...]))
    hms = jnp.mean(hres * hres, axis=-1, keepdims=True)
    y_ref[...] = (hres * lax.rsqrt(hms + EPS) * final_g_ref[...]).reshape(nb, tbs, D_MODEL)

    @pl.when(j == n_t - 1)
    def _():
        for b in range(nb):
            for s in range(CONV_SLABS):
                convn_ref[b, :, s * LANES:(s + 1) * LANES] = (
                    xbc_s[b, s, CONV_HIST - (CONV_W - 1):CONV_HIST, :])
            for s in range(POOL_SLABS):
                pooln_ref[b, :, s * LANES:(s + 1) * LANES] = (
                    u_s[b, s, POOL_HIST - (POOL_MAX - 1):POOL_HIST, :])
            for g in range(SSD_GROUPS):
                st_t = state_s[b, g].T
                for i in range(SSD_HPG):
                    hn_ref[b, SSD_HPG * g + i] = st_t[i * SSD_HEAD_DIM:(i + 1) * SSD_HEAD_DIM]


def _const_spec(shape):
    zeros = (0,) * len(shape)
    return pl.BlockSpec(shape, lambda i, j: zeros, pipeline_mode=pl.Buffered(1))


def _run_layer(x, h0, conv0, pool0, weights, *, nb, tbs, t_valid, pos0, name):
    n_seq, t_pad, _ = x.shape
    assert n_seq % nb == 0 and t_pad % tbs == 0 and tbs % (2 * SUBLANES) == 0
    assert t_valid == tbs or t_pad == tbs
    assert t_valid >= POOL_MAX - 1 and tbs % min(SSD_CHUNK, tbs) == 0
    m = nb * tbs
    grid = (n_seq // nb, t_pad // tbs)

    seq_spec = lambda shape, **kw: pl.BlockSpec(
        (nb,) + shape, lambda i, j: (i,) + (0,) * len(shape), **kw)
    once = dict(pipeline_mode=pl.Buffered(1))
    in_specs = [
        pl.BlockSpec((nb, tbs, D_MODEL), lambda i, j: (i, j, 0)),
        seq_spec((SSD_HEADS, SSD_HEAD_DIM, D_STATE), **once),
        seq_spec((CONV_W - 1, CONV_DIM), **once),
        seq_spec((POOL_MAX - 1, D_POOL), **once),
    ] + [_const_spec(w.shape) for w in weights]
    out_specs = [
        pl.BlockSpec((nb, tbs, D_MODEL), lambda i, j: (i, j, 0)),
        seq_spec((SSD_HEADS, SSD_HEAD_DIM, D_STATE)),
        seq_spec((CONV_W - 1, CONV_DIM)),
        seq_spec((POOL_MAX - 1, D_POOL)),
    ]
    out_shape = [
        jax.ShapeDtypeStruct((n_seq, t_pad, D_MODEL), F32),
        jax.ShapeDtypeStruct((n_seq, SSD_HEADS, SSD_HEAD_DIM, D_STATE), F32),
        jax.ShapeDtypeStruct((n_seq, CONV_W - 1, CONV_DIM), F32),
        jax.ShapeDtypeStruct((n_seq, POOL_MAX - 1, D_POOL), F32),
    ]
    scratch = [
        pltpu.VMEM((nb, CONV_SLABS, CONV_HIST + tbs, LANES), F32),
        pltpu.VMEM((CONV_SLABS, m, LANES), F32),
        pltpu.VMEM((nb, POOL_SLABS, POOL_HIST + tbs, LANES), F32),
        pltpu.VMEM((POOL_SLABS, m, LANES), F32),
        pltpu.VMEM((nb, SSD_GROUPS, D_STATE, GROUP_W), F32),
        pltpu.VMEM((m, D_INNER), F32),
        pltpu.VMEM((m, D_INNER), BF16),
        pltpu.VMEM((m, D_INNER), F32),
        pltpu.VMEM((m, D_POOL), F32),
        pltpu.VMEM((m, 2 * D_MODEL), F32),
    ]
    kern = functools.partial(_layer_kernel, nb=nb, tbs=tbs, t_valid=t_valid, pos0=pos0)
    return pl.pallas_call(
        kern,
        out_shape=out_shape,
        grid=grid,
        in_specs=in_specs,
        out_specs=out_specs,
        scratch_shapes=scratch,
        compiler_params=pltpu.CompilerParams(
            dimension_semantics=("arbitrary", "arbitrary"),
            vmem_limit_bytes=VMEM_LIMIT_BYTES),
        name=name,
    )(x, h0, conv0, pool0, *weights)


def _pack_kernel(w_ref, o_ref):
    o_ref[...] = pltpu.bitcast(w_ref[...].astype(BF16), jnp.uint32)


def _pack_rows(w, col0=0, ncols=None):
    k, n_all = w.shape
    ncols = n_all - col0 if ncols is None else ncols
    bn = min(PACK_COLS, ncols)
    assert col0 % bn == 0 and ncols % bn == 0 and k % (2 * SUBLANES) == 0
    return pl.pallas_call(
        _pack_kernel,
        out_shape=jax.ShapeDtypeStruct((k // 2, ncols), jnp.uint32),
        grid=(ncols // bn,),
        in_specs=[pl.BlockSpec((k, bn), lambda j: (0, j + col0 // bn))],
        out_specs=pl.BlockSpec((k // 2, bn), lambda j: (0, j)),
        compiler_params=pltpu.CompilerParams(dimension_semantics=("arbitrary",)),
        name="pack_weight",
    )(w)


def _prepare_weights(norm_g, w_in, conv_w, conv_b, dt_bias, a_log, d_skip, ssd_norm_g, w_branch_ssd,
                     pool_w, pool_scale, w_branch_pool, gate_b, w_out, final_g):
    row = lambda v: v.reshape(1, -1).astype(F32)
    pad_heads = lambda v: jnp.pad(v.astype(F32), ((0, 0), (0, DT_PAD - SSD_HEADS)))
    return [
        row(norm_g),
        _pack_rows(w_in, ZX_Z0, ZX_COLS),
        _pack_rows(pad_heads(w_in[:, IN_DT0:IN_UPG0])),
        _pack_rows(w_in[:, IN_UPG0:IN_UPG0 + UPG_COLS]),
        conv_w.astype(F32),
        row(conv_b),
        pad_heads(dt_bias.reshape(1, -1)),
        pad_heads(a_log.reshape(1, -1)),
        row(jnp.repeat(d_skip, SSD_HEAD_DIM)),
        row(ssd_norm_g),
        _pack_rows(w_branch_ssd),
        _pack_rows(pool_w.reshape(D_POOL, POOL_GDIM)),
        row(pool_scale),
        _pack_rows(w_branch_pool),
        row(gate_b),
        _pack_rows(w_out),
        row(final_g),
    ]


def kernel(x_prompt, x_sample, state_ssd, state_conv, state_pool, norm_g, w_in, conv_w, conv_b, dt_bias, a_log, d_skip, ssd_norm_g, w_branch_ssd, pool_w, pool_scale, w_branch_pool, gate_b, w_out, final_g):
    depth = w_in.shape[0]
    assert depth == 1, "single-layer step"
    weights = _prepare_weights(norm_g[0], w_in[0], conv_w[0], conv_b[0], dt_bias[0], a_log[0], d_skip[0],
                               ssd_norm_g[0], w_branch_ssd[0], pool_w[0], pool_scale[0],
                               w_branch_pool[0], gate_b[0], w_out[0], final_g)
    bp, tp, _ = x_prompt.shape
    bs, ts, _ = x_sample.shape

    yp, hp, cp, pp = _run_layer(
        x_prompt,
        jnp.zeros((bp, SSD_HEADS, SSD_HEAD_DIM, D_STATE), F32),
        jnp.zeros((bp, CONV_W - 1, CONV_DIM), F32),
        jnp.zeros((bp, POOL_MAX - 1, D_POOL), F32),
        weights, nb=1, tbs=256, t_valid=256, pos0=0, name="layer_prompt")

    ts_pad = SSD_CHUNK
    xs_pad = jnp.pad(x_sample, ((0, 0), (0, ts_pad - ts), (0, 0)))
    ys, hs, cs, ps = _run_layer(
        xs_pad, state_ssd[0], state_conv[0], state_pool[0],
        weights, nb=1, tbs=ts_pad, t_valid=ts, pos0=PAST_LEN, name="layer_sample")

    return (yp, ys[:, :ts], hp[None], hs[None], cp[None], cs[None], pp[None], ps[None])
```
